```python
import math
import jax, jax.numpy as jnp
from jax import lax
import numpy as np

D_MODEL = 2048
BATCH = 2
SEQ = 4096
DEPTH = 4
DEC_BATCH = 32
DEC_SEQ = 4
PAST_LEN = 16384
PAGE_SIZE = 128

MIX_W = D_MODEL // 2
N_BRANCH = 3
HEAD_DIM_A = 64
N_HEADS_A = MIX_W // HEAD_DIM_A
N_KV_A = N_HEADS_A // 4
GROUP_A = N_HEADS_A // N_KV_A
WINDOW = 128
ROT_DIM = HEAD_DIM_A // 4
ROPE_THETA = 500000.0
N_HEADS_B = 4
DV_B = MIX_W // N_HEADS_B
DK_B = DV_B // 2
GATE_RANK = 16
GATE_TAU = 16.0
DK_C = 128
DV_C = 128
N_HEADS_C = MIX_W // DV_C
CONV_C = 4
QKV_C = N_HEADS_C * (2 * DK_C + DV_C)
D_FF = 256 * ((8 * D_MODEL // 3 + 255) // 256)
FFN_CONV = 3
CHUNK = 64
LN_EPS = 1e-5
NORM_EPS = 1e-6
ALPHA = (2 * DEPTH) ** 0.25
BETA = (8 * DEPTH) ** -0.25
IN_SPLITS = (N_HEADS_A * HEAD_DIM_A, N_KV_A * HEAD_DIM_A, N_KV_A * HEAD_DIM_A,
             N_HEADS_B * DK_B, N_HEADS_B * DK_B, N_HEADS_B * DV_B, N_HEADS_B * DV_B, GATE_RANK,
             QKV_C, N_HEADS_C * DV_C, N_HEADS_C, N_HEADS_C,
             N_BRANCH * D_MODEL)
IN_COLS = sum(IN_SPLITS)

kernel_name = 'hybrid_swa_gla_gdn_convffn_step'


def split_proj(proj):
    idx = np.cumsum(IN_SPLITS)[:-1].tolist()
    return jnp.split(proj, idx, axis=-1)


def layer_norm(x, g, b):
    xf = x.astype(jnp.float32)
    mu = jnp.mean(xf, -1, keepdims=True)
    var = jnp.mean(jnp.square(xf - mu), -1, keepdims=True)
    return ((xf - mu) * lax.rsqrt(var + LN_EPS) * g + b).astype(x.dtype)


def rms_norm(x, g):
    xf = x.astype(jnp.float32)
    return xf * lax.rsqrt(jnp.mean(xf * xf, -1, keepdims=True) + NORM_EPS) * g


def l2norm(x):
    return x * lax.rsqrt(jnp.sum(x * x, -1, keepdims=True) + NORM_EPS)


def rotary(x, pos):
    half = ROT_DIM // 2
    inv = ROPE_THETA ** (-jnp.arange(half, dtype=jnp.float32) / half)
    ang = pos.astype(jnp.float32)[:, None] * inv[None, :]
    cos = jnp.cos(ang)[None, :, None, :]
    sin = jnp.sin(ang)[None, :, None, :]
    xr = x[..., :ROT_DIM].astype(jnp.float32)
    x1, x2 = xr[..., :half], xr[..., half:]
    rot = jnp.concatenate([x1 * cos - x2 * sin, x2 * cos + x1 * sin], -1)
    return jnp.concatenate([rot.astype(x.dtype), x[..., ROT_DIM:]], -1)


def sink_softmax(s, mask, sinks):
    s = jnp.where(mask, s, -jnp.inf)
    sk = sinks.astype(jnp.float32).reshape(N_KV_A, GROUP_A, 1, 1)
    m = jnp.maximum(jnp.max(s, -1, keepdims=True), sk)
    p = jnp.exp(s - m)
    return p / (jnp.sum(p, -1, keepdims=True) + jnp.exp(sk - m))


def swa_prompt(q, k, v, sinks):
    B, T = q.shape[:2]
    nb = -(-T // WINDOW)
    tp = nb * WINDOW
    padt = ((0, 0), (0, tp - T), (0, 0), (0, 0))
    q, k, v = jnp.pad(q, padt), jnp.pad(k, padt), jnp.pad(v, padt)
    qb = q.reshape(B, nb, WINDOW, N_KV_A, GROUP_A, HEAD_DIM_A)

    def band(a):
        prev = jnp.pad(a, ((0, 0), (WINDOW, 0), (0, 0), (0, 0)))[:, :tp]
        return jnp.concatenate([prev.reshape(B, nb, WINDOW, N_KV_A, HEAD_DIM_A),
                                a.reshape(B, nb, WINDOW, N_KV_A, HEAD_DIM_A)], axis=2)

    kb, vb = band(k), band(v)
    i = jnp.arange(WINDOW)[:, None]
    j = jnp.arange(2 * WINDOW)[None, :]
    blk = jnp.arange(nb)[:, None, None]
    mask = (j >= i) & (j <= i + WINDOW) & ((blk > 0) | (j >= WINDOW))
    s = jnp.einsum('bnqkgd,bnskd->bnkgqs', qb, kb,
                   preferred_element_type=jnp.float32) * HEAD_DIM_A ** -0.5
    p = sink_softmax(s, mask[:, None, None], sinks)
    o = jnp.einsum('bnkgqs,bnskd->bnqkgd', p.astype(v.dtype), vb)
    return o.reshape(B, tp, N_HEADS_A * HEAD_DIM_A)[:, :T]


def swa_decode(q, k, v, ck, cv, sinks):
    B, T = q.shape[:2]
    cw = ck.shape[1]
    kk = jnp.concatenate([ck.astype(k.dtype), k], axis=1)
    vv = jnp.concatenate([cv.astype(v.dtype), v], axis=1)
    diff = jnp.arange(T)[:, None] - (jnp.arange(cw + T)[None, :] - cw)
    mask = (diff >= 0) & (diff <= WINDOW)
    qg = q.reshape(B, T, N_KV_A, GROUP_A, HEAD_DIM_A)
    s = jnp.einsum('btkgd,bskd->bkgts', qg, kk,
                   preferred_element_type=jnp.float32) * HEAD_DIM_A ** -0.5
    p = sink_softmax(s, mask, sinks)
    o = jnp.einsum('bkgts,bskd->btkgd', p.astype(vv.dtype), vv)
    return o.reshape(B, T, N_HEADS_A * HEAD_DIM_A), kk[:, -cw:], vv[:, -cw:]


def to_chunks(a, c, n):
    pad = n * c - a.shape[1]
    a = jnp.pad(a, [(0, 0), (0, pad)] + [(0, 0)] * (a.ndim - 2))
    return a.reshape(a.shape[0], n, c, *a.shape[2:])


def gla_chunked(q, k, v, logdec, s0):
    B, T = q.shape[:2]
    c = min(CHUNK, T)
    n = -(-T // c)
    q, k, v, logdec = (to_chunks(a, c, n) for a in (q, k, v, logdec))
    b = jnp.cumsum(logdec, axis=2)
    b_last = b[:, :, -1:]
    qt = q * jnp.exp(b)
    kt = k * jnp.exp(-b)
    k_end = k * jnp.exp(b_last - b)
    causal = jnp.tril(jnp.ones((c, c), bool))
    a = jnp.where(causal, jnp.einsum('bnchd,bnshd->bnhcs', qt, kt), 0.0)
    o_intra = jnp.einsum('bnhcs,bnshv->bnchv', a, v)

    def step(s, inp):
        qt_i, ke_i, v_i, dec_i = inp
        o_i = jnp.einsum('bchd,bhdv->bchv', qt_i, s)
        s = s * dec_i[..., None] + jnp.einsum('bchd,bchv->bhdv', ke_i, v_i)
        return s, o_i

    xs = (jnp.moveaxis(qt, 1, 0), jnp.moveaxis(k_end, 1, 0), jnp.moveaxis(v, 1, 0),
          jnp.moveaxis(jnp.exp(b_last[:, :, 0]), 1, 0))
    s_fin, o_inter = lax.scan(step, s0, xs)
    o = o_intra + jnp.moveaxis(o_inter, 0, 1)
    return o.reshape(B, n * c, *o.shape[3:])[:, :T], s_fin


def gated_delta_chunked(q, k, v, g, beta, s0):
    B, T, H = q.shape[:3]
    dv = v.shape[-1]
    c = min(CHUNK, T)
    n = -(-T // c)
    q, k, v, g, beta = (to_chunks(a, c, n) for a in (q, k, v, g, beta))
    gh = jnp.moveaxis(jnp.cumsum(g, axis=2), 3, 2)
    diff = gh[..., :, None] - gh[..., None, :]
    t = jnp.arange(c)
    gam_strict = jnp.exp(jnp.where(t[:, None] > t[None, :], diff, -jnp.inf))
    gam_incl = jnp.exp(jnp.where(t[:, None] >= t[None, :], diff, -jnp.inf))
    bh = jnp.moveaxis(beta, 3, 2)
    qh, kh, vh = (jnp.moveaxis(a, 3, 2) for a in (q, k, v))
    kk = jnp.einsum('bnhcd,bnhsd->bnhcs', kh, kh)
    lmat = jnp.eye(c, dtype=kk.dtype) + bh[..., None] * kk * gam_strict
    rhs = jnp.concatenate([vh * bh[..., None], kh * (bh * jnp.exp(gh))[..., None]], axis=-1)
    sol = lax.linalg.triangular_solve(lmat, rhs, left_side=True, lower=True, unit_diagonal=True)
    u_base, w = sol[..., :dv], sol[..., dv:]
    a_qk = jnp.einsum('bnhcd,bnhsd->bnhcs', qh, kh) * gam_incl
    q_dec = qh * jnp.exp(gh)[..., None]
    g_last = gh[..., -1:]
    k_end = kh * jnp.exp(g_last - gh)[..., None]
    dec = jnp.exp(g_last[..., 0])

    def step(s, inp):
        ub, w_i, qd, aq, ke, d = inp
        u = ub - jnp.einsum('bhcd,bhdv->bhcv', w_i, s)
        o = jnp.einsum('bhcd,bhdv->bhcv', qd, s) + jnp.einsum('bhcs,bhsv->bhcv', aq, u)
        s = s * d[..., None, None] + jnp.einsum('bhcd,bhcv->bhdv', ke, u)
        return s, o

    xs = tuple(jnp.moveaxis(a, 1, 0) for a in (u_base, w, q_dec, a_qk, k_end, dec))
    s_fin, o = lax.scan(step, s0, xs)
    o = jnp.transpose(o, (1, 0, 3, 2, 4)).reshape(B, n * c, H, dv)[:, :T]
    return o, s_fin


def causal_conv(x, buf, w):
    width = w.shape[0]
    T = x.shape[1]
    xx = jnp.concatenate([buf.astype(x.dtype), x], axis=1)
    y = xx[:, 0:T] * w[0]
    for j in range(1, width):
        y = y + xx[:, j:j + T] * w[j]
    return y, xx[:, T:]


def decoder_layer(x, pos0, p, swa_kv, gla_s, dn_s, dn_buf, ff_buf):
    B, T, _ = x.shape
    f32 = jnp.float32
    aq, ak, av, bq, bk, bv, bg, blr, cqkv, cz, ca, cb, mg = split_proj(x @ p['w_in'])
    pos = pos0 + jnp.arange(T)
    q = rotary(aq.reshape(B, T, N_HEADS_A, HEAD_DIM_A), pos)
    k = rotary(ak.reshape(B, T, N_KV_A, HEAD_DIM_A), pos)
    v = av.reshape(B, T, N_KV_A, HEAD_DIM_A)
    if swa_kv is None:
        o_a = swa_prompt(q, k, v, p['sinks'])
        new_k, new_v = k[:, -WINDOW:], v[:, -WINDOW:]
    else:
        o_a, new_k, new_v = swa_decode(q, k, v, swa_kv[0], swa_kv[1], p['sinks'])
    gq = bq.reshape(B, T, N_HEADS_B, DK_B).astype(f32) * DK_B ** -0.5
    gk = bk.reshape(B, T, N_HEADS_B, DK_B).astype(f32)
    gv = bv.reshape(B, T, N_HEADS_B, DV_B).astype(f32)
    glog = jax.nn.log_sigmoid((blr @ p['w_gla_gate'] + p['b_gla_gate']).astype(f32)) / GATE_TAU
    s0_b = jnp.zeros((B, N_HEADS_B, DK_B, DV_B), f32) if gla_s is None else gla_s.astype(f32)
    o_b, s_b = gla_chunked(gq, gk, gv, glog.reshape(B, T, N_HEADS_B, DK_B), s0_b)
    o_b = rms_norm(o_b, p['gla_norm']) * jax.nn.silu(bg.reshape(B, T, N_HEADS_B, DV_B).astype(f32))
    o_b = o_b.reshape(B, T, MIX_W).astype(x.dtype)
    buf = jnp.zeros((B, CONV_C - 1, QKV_C), x.dtype) if dn_buf is None else dn_buf
    c_conv, new_dbuf = causal_conv(cqkv, buf, p['dn_conv'])
    c_conv = jax.nn.silu(c_conv.astype(f32))
    dq, dk, dvv = jnp.split(c_conv, [N_HEADS_C * DK_C, 2 * N_HEADS_C * DK_C], axis=-1)
    dq = l2norm(dq.reshape(B, T, N_HEADS_C, DK_C)) * DK_C ** -0.5
    dk = l2norm(dk.reshape(B, T, N_HEADS_C, DK_C))
    dvv = dvv.reshape(B, T, N_HEADS_C, DV_C)
    g = -jnp.exp(p['dn_a_log'].astype(f32)) * jax.nn.softplus(ca.astype(f32) + p['dn_dt_bias'].astype(f32))
    beta = jax.nn.sigmoid(cb.astype(f32))
    s0_c = jnp.zeros((B, N_HEADS_C, DK_C, DV_C), f32) if dn_s is None else dn_s.astype(f32)
    o_c, s_c = gated_delta_chunked(dq, dk, dvv, g, beta, s0_c)
    o_c = rms_norm(o_c, p['dn_norm']) * jax.nn.silu(cz.reshape(B, T, N_HEADS_C, DV_C).astype(f32))
    o_c = o_c.reshape(B, T, MIX_W).astype(x.dtype)
    branches = jnp.stack([o_a, o_b, o_c], axis=2)
    gates = jax.nn.sigmoid(mg.reshape(B, T, N_BRANCH, D_MODEL))
    mixed = jnp.sum(gates * jnp.einsum('btnc,ncd->btnd', branches, p['w_branch']), axis=2)
    h = layer_norm(ALPHA * x + mixed @ p['w_out'], p['ln1_g'], p['ln1_b'])
    u = h @ p['w_up']
    fbuf = jnp.zeros((B, FFN_CONV - 1, 2 * D_FF), x.dtype) if ff_buf is None else ff_buf
    uc, new_fbuf = causal_conv(u, fbuf, p['ffn_conv'])
    gate, up = jnp.split(uc, 2, axis=-1)
    f = (jax.nn.silu(gate) * up) @ p['w_down']
    y = layer_norm(ALPHA * h + f, p['ln2_g'], p['ln2_b'])
    return y, (new_k, new_v, s_b, s_c, new_dbuf, new_fbuf)


def setup_inputs(seed: int = 0) -> dict:
    key = jax.random.key(seed)
    ks = jax.random.split(key, 32)
    f32 = jnp.float32

    def nrm(k, shape, s):
        return jax.random.normal(k, shape, f32) * s

    cache_w = min(WINDOW, PAST_LEN)
    dt = jnp.exp(jax.random.uniform(ks[13], (DEPTH, N_HEADS_C), f32, math.log(1e-3), math.log(1e-1)))
    return {
        'x_prompt': nrm(ks[0], (BATCH, SEQ, D_MODEL), 1.0),
        'x_sample': nrm(ks[1], (DEC_BATCH, DEC_SEQ, D_MODEL), 1.0),
        'cache_swa_k': nrm(ks[2], (DEPTH, DEC_BATCH, cache_w, N_KV_A, HEAD_DIM_A), 1.0),
        'cache_swa_v': nrm(ks[3], (DEPTH, DEC_BATCH, cache_w, N_KV_A, HEAD_DIM_A), 1.0),
        'state_gla': nrm(ks[4], (DEPTH, DEC_BATCH, N_HEADS_B, DK_B, DV_B), 1.0),
        'state_delta': nrm(ks[5], (DEPTH, DEC_BATCH, N_HEADS_C, DK_C, DV_C), DK_C ** -0.5),
        'state_delta_conv': nrm(ks[6], (DEPTH, DEC_BATCH, CONV_C - 1, QKV_C), 1.0),
        'state_ffn_conv': nrm(ks[7], (DEPTH, DEC_BATCH, FFN_CONV - 1, 2 * D_FF), 1.0),
        'w_in': nrm(ks[8], (DEPTH, D_MODEL, IN_COLS), D_MODEL ** -0.5),
        'attn_sinks': nrm(ks[9], (DEPTH, N_HEADS_A), 1.0),
        'w_gla_gate': nrm(ks[10], (DEPTH, GATE_RANK, N_HEADS_B * DK_B), GATE_RANK ** -0.5),
        'b_gla_gate': nrm(ks[11], (DEPTH, N_HEADS_B * DK_B), 0.1),
        'gla_norm': 1.0 + nrm(ks[12], (DEPTH, DV_B), 0.02),
        'dn_conv': nrm(ks[14], (DEPTH, CONV_C, QKV_C), CONV_C ** -0.5),
        'dn_a_log': jnp.log(jax.random.uniform(ks[15], (DEPTH, N_HEADS_C), f32, 1.0, 16.0)),
        'dn_dt_bias': dt + jnp.log(-jnp.expm1(-dt)),
        'dn_norm': 1.0 + nrm(ks[16], (DEPTH, DV_C), 0.02),
        'w_branch': nrm(ks[17], (DEPTH, N_BRANCH, MIX_W, D_MODEL), MIX_W ** -0.5),
        'w_out': nrm(ks[18], (DEPTH, D_MODEL, D_MODEL), D_MODEL ** -0.5 * BETA),
        'ln1_g': 1.0 + nrm(ks[19], (DEPTH, D_MODEL), 0.02),
        'ln1_b': nrm(ks[20], (DEPTH, D_MODEL), 0.02),
        'w_up': nrm(ks[21], (DEPTH, D_MODEL, 2 * D_FF), D_MODEL ** -0.5),
        'ffn_conv': nrm(ks[22], (DEPTH, FFN_CONV, 2 * D_FF), FFN_CONV ** -0.5),
        'w_down': nrm(ks[23], (DEPTH, D_FF, D_MODEL), D_FF ** -0.5 * BETA),
        'ln2_g': 1.0 + nrm(ks[24], (DEPTH, D_MODEL), 0.02),
        'ln2_b': nrm(ks[25], (DEPTH, D_MODEL), 0.02),
    }


def reference(x_prompt, x_sample, cache_swa_k, cache_swa_v, state_gla, state_delta,
              state_delta_conv, state_ffn_conv, w_in, attn_sinks, w_gla_gate, b_gla_gate,
              gla_norm, dn_conv, dn_a_log, dn_dt_bias, dn_norm, w_branch, w_out,
              ln1_g, ln1_b, w_up, ffn_conv, w_down, ln2_g, ln2_b):
    y_p = x_prompt
    y_s = x_sample
    new_p = []
    new_s = []
    for l in range(DEPTH):
        p = {'w_in': w_in[l], 'sinks': attn_sinks[l], 'w_gla_gate': w_gla_gate[l],
             'b_gla_gate': b_gla_gate[l], 'gla_norm': gla_norm[l], 'dn_conv': dn_conv[l],
             'dn_a_log': dn_a_log[l], 'dn_dt_bias': dn_dt_bias[l], 'dn_norm': dn_norm[l],
             'w_branch': w_branch[l], 'w_out': w_out[l], 'ln1_g': ln1_g[l], 'ln1_b': ln1_b[l],
             'w_up': w_up[l], 'ffn_conv': ffn_conv[l], 'w_down': w_down[l],
             'ln2_g': ln2_g[l], 'ln2_b': ln2_b[l]}
        y_p, st_p = decoder_layer(y_p, 0, p, None, None, None, None, None)
        y_s, st_s = decoder_layer(y_s, PAST_LEN, p, (cache_swa_k[l], cache_swa_v[l]),
                                  state_gla[l], state_delta[l], state_delta_conv[l], state_ffn_conv[l])
        new_p.append(st_p)
        new_s.append(st_s)
    p_k, p_v, p_gla, p_delta, p_dconv, p_fconv = (jnp.stack(a) for a in zip(*new_p))
    s_k, s_v, s_gla, s_delta, s_dconv, s_fconv = (jnp.stack(a) for a in zip(*new_s))
    return (y_p, y_s, p_k, p_v, p_gla, p_delta, p_dconv, p_fconv,
            s_k, s_v, s_gla, s_delta, s_dconv, s_fconv)
```

```python
import functools

import numpy as np
import jax
import jax.numpy as jnp
from jax import lax
from jax.experimental import pallas as pl
from jax.experimental.pallas import tpu as pltpu

F32 = jnp.float32
BF16 = jnp.bfloat16
HIGHEST = lax.Precision.HIGHEST

D_MODEL = 2048
BATCH = 2
SEQ = 4096
DEPTH = 4
DEC_BATCH = 32
DEC_SEQ = 4
PAST_LEN = 16384
MIX_W = D_MODEL // 2
N_BRANCH = 3
HEAD_DIM_A = 64
N_HEADS_A = MIX_W // HEAD_DIM_A
N_KV_A = N_HEADS_A // 4
GROUP_A = N_HEADS_A // N_KV_A
KV_W = N_KV_A * HEAD_DIM_A
WINDOW = 128
ROT_DIM = HEAD_DIM_A // 4
ROPE_THETA = 500000.0
N_HEADS_B = 4
DV_B = MIX_W // N_HEADS_B
DK_B = DV_B // 2
GATE_RANK = 16
GATE_TAU = 16.0
DK_C = 128
DV_C = 128
N_HEADS_C = MIX_W // DV_C
CONV_C = 4
QKV_C = N_HEADS_C * (2 * DK_C + DV_C)
D_FF = 256 * ((8 * D_MODEL // 3 + 255) // 256)
FFN_CONV = 3
CHUNK = 64
LN_EPS = 1e-5
NORM_EPS = 1e-6
ALPHA = (2 * DEPTH) ** 0.25
IN_SPLITS = (N_HEADS_A * HEAD_DIM_A, KV_W, KV_W,
             N_HEADS_B * DK_B, N_HEADS_B * DK_B, N_HEADS_B * DV_B, N_HEADS_B * DV_B, GATE_RANK,
             QKV_C, N_HEADS_C * DV_C, N_HEADS_C, N_HEADS_C,
             N_BRANCH * D_MODEL)

LANES = 128
SUBLANES = 8
VMEM_LIMIT = 56 * 1024 * 1024

COL_MG = 0
COL_CQKV = COL_MG + N_BRANCH * D_MODEL
COL_AQ = COL_CQKV + QKV_C
COL_BV = COL_AQ + MIX_W
COL_BG = COL_BV + MIX_W
COL_CZ = COL_BG + MIX_W
COL_BQ = COL_CZ + MIX_W
COL_BK = COL_BQ + N_HEADS_B * DK_B
COL_AK = COL_BK + N_HEADS_B * DK_B
COL_AV = COL_AK + KV_W
COL_SM = COL_AV + KV_W
N_PROJ = 15360
LANE_BLR = 0
LANE_CA = GATE_RANK
LANE_CB = GATE_RANK + N_HEADS_C

GROUP = 8
TOK0 = 3
DEC_ROWS = DEC_BATCH * GROUP


def _cparams(*sem):
    return pltpu.CompilerParams(dimension_semantics=sem, vmem_limit_bytes=VMEM_LIMIT)


def _sigmoid(x):
    return 1.0 / (1.0 + jnp.exp(-x))


def _silu(x):
    return x * _sigmoid(x)


def _softplus(x):
    return jnp.maximum(x, 0.0) + jnp.log1p(jnp.exp(-jnp.abs(x)))


def _dot(a, b):
    return jnp.dot(a, b, preferred_element_type=F32)


def _dot_nt(a, b):
    return lax.dot_general(a, b, (((1,), (1,)), ((), ())), preferred_element_type=F32)


def _dot_tn(a, b):
    return lax.dot_general(a, b, (((0,), (0,)), ((), ())), preferred_element_type=F32)


def _layer_norm(x, g, b):
    mu = jnp.mean(x, -1, keepdims=True)
    xc = x - mu
    var = jnp.mean(xc * xc, -1, keepdims=True)
    return xc * lax.rsqrt(var + LN_EPS) * g + b


def _row_mask(c, valid):
    r = lax.broadcasted_iota(jnp.int32, (c, 1), 0)
    return (r >= valid[0]) & (r < valid[1])


def _mm_kernel(a_ref, w_ref, o_ref):
    o_ref[...] = _dot(a_ref[...], w_ref[...]).astype(o_ref.dtype)


def _matmul(a, w, l, tm, tn, out_dtype=F32):
    m, k = a.shape
    n = w.shape[-1]
    return pl.pallas_call(
        _mm_kernel,
        grid=(m // tm, n // tn),
        in_specs=[pl.BlockSpec((tm, k), lambda i, j: (i, 0)),
                  pl.BlockSpec((None, k, tn), lambda i, j: (l, 0, j))],
        out_specs=pl.BlockSpec((tm, tn), lambda i, j: (i, j)),
        out_shape=jax.ShapeDtypeStruct((m, n), out_dtype),
        compiler_params=_cparams("parallel", "parallel"),
        name="in_proj",
    )(a, w)


def _rotary(x, tab):
    cols = []
    for p in range(x.shape[1] // LANES):
        xp = x[:, LANES * p:LANES * (p + 1)]
        cols.append(xp * tab[0]
                    + pltpu.roll(xp, LANES - ROT_DIM // 2, 1) * tab[1]
                    + pltpu.roll(xp, ROT_DIM // 2, 1) * tab[2])
    return cols[0] if len(cols) == 1 else jnp.concatenate(cols, axis=1)


def _rope_tables(pos):
    half = ROT_DIM // 2
    r = pos.shape[0]
    inv = ROPE_THETA ** (-jnp.arange(half, dtype=F32) / half)
    ang = pos.astype(F32)[:, None] * inv[None, :]
    cos, sin = jnp.cos(ang), jnp.sin(ang)
    rest = HEAD_DIM_A - ROT_DIM
    c64 = jnp.concatenate([cos, cos, jnp.ones((r, rest), F32)], 1)
    sa64 = jnp.concatenate([-sin, jnp.zeros((r, HEAD_DIM_A - half), F32)], 1)
    sb64 = jnp.concatenate([jnp.zeros((r, half), F32), sin, jnp.zeros((r, rest), F32)], 1)
    rep = LANES // HEAD_DIM_A
    return jnp.stack([jnp.tile(c64, (1, rep)), jnp.tile(sa64, (1, rep)), jnp.tile(sb64, (1, rep))])


def _sink_attention(q, kg, vg, mask, sink):
    s = _dot_nt(q, kg) * HEAD_DIM_A ** -0.5
    s = jnp.where(mask, s, -jnp.inf)
    mx = jnp.maximum(jnp.max(s, -1, keepdims=True), sink)
    p = jnp.exp(s - mx)
    den = jnp.sum(p, -1, keepdims=True) + jnp.exp(sink - mx)
    return _dot(p.astype(BF16), vg) / den


def _swa_prompt_kernel(sinks_ref, q_ref, kc_ref, kp_ref, vc_ref, vp_ref, tc_ref, tp_ref,
                       o_ref, kout_ref, vout_ref, *, l):
    n = pl.program_id(1)
    tc = tc_ref[...]
    q = _rotary(q_ref[...], tc)
    kc = _rotary(kc_ref[...], tc)
    kp = _rotary(kp_ref[...], tp_ref[...])
    vc = vc_ref[...]
    kout_ref[...] = kc
    vout_ref[...] = vc
    kall = jnp.concatenate([kp, kc], axis=0).astype(BF16)
    vall = jnp.concatenate([vp_ref[...], vc], axis=0).astype(BF16)
    i = lax.broadcasted_iota(jnp.int32, (WINDOW, 2 * WINDOW), 0)
    j = lax.broadcasted_iota(jnp.int32, (WINDOW, 2 * WINDOW), 1)
    mask = (j >= i) & (j <= i + WINDOW) & ((n > 0) | (j >= WINDOW))
    qb = q.astype(BF16)
    for pair in range(N_HEADS_A // 2):
        outs = []
        for h in (2 * pair, 2 * pair + 1):
            g = h // GROUP_A
            kg = kall[:, HEAD_DIM_A * g:HEAD_DIM_A * (g + 1)]
            vg = vall[:, HEAD_DIM_A * g:HEAD_DIM_A * (g + 1)]
            qh = qb[:, HEAD_DIM_A * h:HEAD_DIM_A * (h + 1)]
            outs.append(_sink_attention(qh, kg, vg, mask, sinks_ref[l, h]))
        o_ref[:, LANES * pair:LANES * (pair + 1)] = jnp.concatenate(outs, axis=1).astype(o_ref.dtype)


def _swa_prompt(proj, tabs, sinks, l, nb_seq, t):
    nb = t // WINDOW
    m = nb_seq * t
    qc = COL_AQ // MIX_W
    kcol = COL_AK // KV_W
    vcol = COL_AV // KV_W
    cur = lambda b, n: b * nb + n
    prev = lambda b, n: b * nb + jnp.maximum(n - 1, 0)
    return pl.pallas_call(
        functools.partial(_swa_prompt_kernel, l=l),
        grid=(nb_seq, nb),
        in_specs=[
            pl.BlockSpec(memory_space=pltpu.SMEM),
            pl.BlockSpec((WINDOW, MIX_W), lambda b, n: (cur(b, n), qc)),
            pl.BlockSpec((WINDOW, KV_W), lambda b, n: (cur(b, n), kcol)),
            pl.BlockSpec((WINDOW, KV_W), lambda b, n: (prev(b, n), kcol)),
            pl.BlockSpec((WINDOW, KV_W), lambda b, n: (cur(b, n), vcol)),
            pl.BlockSpec((WINDOW, KV_W), lambda b, n: (prev(b, n), vcol)),
            pl.BlockSpec((3, WINDOW, LANES), lambda b, n: (0, n, 0)),
            pl.BlockSpec((3, WINDOW, LANES), lambda b, n: (0, jnp.maximum(n - 1, 0), 0)),
        ],
        out_specs=[
            pl.BlockSpec((WINDOW, MIX_W), lambda b, n: (cur(b, n), 0)),
            pl.BlockSpec((None, WINDOW, KV_W), lambda b, n: (b, 0, 0)),
            pl.BlockSpec((None, WINDOW, KV_W), lambda b, n: (b, 0, 0)),
        ],
        out_shape=[
            jax.ShapeDtypeStruct((m, MIX_W), BF16),
            jax.ShapeDtypeStruct((nb_seq, WINDOW, KV_W), F32),
            jax.ShapeDtypeStruct((nb_seq, WINDOW, KV_W), F32),
        ],
        compiler_params=_cparams("parallel", "arbitrary"),
        name="swa_prompt",
    )(sinks, proj, proj, proj, proj, proj, tabs, tabs)


SWA_DEC_SEQS = 8


def _swa_decode_kernel(sinks_ref, q_ref, k_ref, v_ref, tab_ref, ck_ref, cv_ref,
                       o_ref, cko_ref, cvo_ref, *, l):
    tab = tab_ref[...]
    cw = ck_ref.shape[1]
    span = 2 * cw
    rows = GROUP_A * GROUP
    r = lax.broadcasted_iota(jnp.int32, (rows, span), 0)
    s = lax.broadcasted_iota(jnp.int32, (rows, span), 1)
    tq = (r % GROUP) - TOK0
    krel = jnp.where(s < cw, s - cw, s - cw - TOK0)
    diff = tq - krel
    key_ok = (s < cw) | ((s >= cw + TOK0) & (s < cw + TOK0 + DEC_SEQ))
    mask = (diff >= 0) & (diff <= WINDOW) & key_ok
    rr = lax.broadcasted_iota(jnp.int32, (rows, 1), 0) // GROUP
    pad = jnp.zeros((span - cw - GROUP, KV_W), F32)
    for bb in range(SWA_DEC_SEQS):
        rs = slice(GROUP * bb, GROUP * (bb + 1))
        tb = tab
        q = _rotary(q_ref[rs, :], tb).astype(BF16)
        kn = _rotary(k_ref[rs, :], tb)
        vn = v_ref[rs, :]
        ck = ck_ref[bb]
        cv = cv_ref[bb]
        kall = jnp.concatenate([ck, kn, pad], axis=0).astype(BF16)
        vall = jnp.concatenate([cv, vn, pad], axis=0).astype(BF16)
        outs = []
        for g in range(N_KV_A):
            kg = kall[:, HEAD_DIM_A * g:HEAD_DIM_A * (g + 1)]
            vg = vall[:, HEAD_DIM_A * g:HEAD_DIM_A * (g + 1)]
            qg = jnp.concatenate(
                [q[:, HEAD_DIM_A * h:HEAD_DIM_A * (h + 1)] for h in range(GROUP_A * g, GROUP_A * (g + 1))], axis=0)
            sink = jnp.zeros((rows, 1), F32)
            for hh in range(GROUP_A):
                sink = jnp.where(rr == hh, sinks_ref[l, GROUP_A * g + hh], sink)
            og = _sink_attention(qg, kg, vg, mask, sink)
            outs.extend(og[GROUP * hh:GROUP * (hh + 1)] for hh in range(GROUP_A))
        o_ref[rs, :] = jnp.concatenate(outs, axis=1).astype(o_ref.dtype)
        keep = cw - DEC_SEQ
        cko_ref[bb, 0:keep, :] = ck_ref[bb, DEC_SEQ:cw, :]
        cko_ref[bb, keep:cw, :] = kn[TOK0:TOK0 + DEC_SEQ]
        cvo_ref[bb, 0:keep, :] = cv_ref[bb, DEC_SEQ:cw, :]
        cvo_ref[bb, keep:cw, :] = vn[TOK0:TOK0 + DEC_SEQ]


def _swa_decode(proj, tab, sinks, cache_k, cache_v, l):
    nseq, cw = cache_k.shape[1], cache_k.shape[2]
    rows = SWA_DEC_SEQS * GROUP
    return pl.pallas_call(
        functools.partial(_swa_decode_kernel, l=l),
        grid=(nseq // SWA_DEC_SEQS,),
        in_specs=[
            pl.BlockSpec(memory_space=pltpu.SMEM),
            pl.BlockSpec((rows, MIX_W), lambda i: (i, COL_AQ // MIX_W)),
            pl.BlockSpec((rows, KV_W), lambda i: (i, COL_AK // KV_W)),
            pl.BlockSpec((rows, KV_W), lambda i: (i, COL_AV // KV_W)),
            pl.BlockSpec((3, GROUP, LANES), lambda i: (0, 0, 0)),
            pl.BlockSpec((None, SWA_DEC_SEQS, cw, KV_W), lambda i: (l, i, 0, 0)),
            pl.BlockSpec((None, SWA_DEC_SEQS, cw, KV_W), lambda i: (l, i, 0, 0)),
        ],
        out_specs=[
            pl.BlockSpec((rows, MIX_W), lambda i: (i, 0)),
            pl.BlockSpec((SWA_DEC_SEQS, cw, KV_W), lambda i: (i, 0, 0)),
            pl.BlockSpec((SWA_DEC_SEQS, cw, KV_W), lambda i: (i, 0, 0)),
        ],
        out_shape=[
            jax.ShapeDtypeStruct((nseq * GROUP, MIX_W), F32),
            jax.ShapeDtypeStruct((nseq, cw, KV_W), F32),
            jax.ShapeDtypeStruct((nseq, cw, KV_W), F32),
        ],
        compiler_params=_cparams("parallel"),
        name="swa_decode",
    )(sinks, proj, proj, proj, tab, cache_k, cache_v)


def _gla_kernel(*refs, c, n, has_state, valid):
    if has_state:
        q_ref, k_ref, v_ref, g_ref, sm_ref, wg_ref, bg_ref, norm_ref, s0_ref, o_ref, sout_ref, st_ref = refs
    else:
        q_ref, k_ref, v_ref, g_ref, sm_ref, wg_ref, bg_ref, norm_ref, o_ref, sout_ref, st_ref = refs
    ci = pl.program_id(1)

    @pl.when(ci == 0)
    def _():
        if has_state:
            for h in range(N_HEADS_B):
                st_ref[h] = s0_ref[h].T
        else:
            st_ref[...] = jnp.zeros_like(st_ref)

    vmask = _row_mask(c, valid) if valid else None
    z = _dot(sm_ref[...].astype(BF16), wg_ref[...]) + bg_ref[...]
    glog = -_softplus(-z) * (1.0 / GATE_TAU)
    if valid:
        glog = jnp.where(vmask, glog, 0.0)
    i = lax.broadcasted_iota(jnp.int32, (c, c), 0)
    j = lax.broadcasted_iota(jnp.int32, (c, c), 1)
    causal = i >= j
    b = jnp.dot(causal.astype(F32), glog, precision=HIGHEST, preferred_element_type=F32)
    blast = b[c - 1:c, :]
    q = q_ref[...] * DK_B ** -0.5
    k = k_ref[...]
    if valid:
        k = jnp.where(vmask, k, 0.0)
    qt = (q * jnp.exp(b)).astype(BF16)
    kt = (k * jnp.exp(-b)).astype(BF16)
    ke = (k * jnp.exp(blast - b)).astype(BF16)
    dec = jnp.exp(blast)
    for h in range(N_HEADS_B):
        ks = slice(DK_B * h, DK_B * (h + 1))
        vs = slice(DV_B * h, DV_B * (h + 1))
        a = jnp.where(causal, _dot_nt(qt[:, ks], kt[:, ks]), 0.0)
        v = v_ref[:, vs]
        if valid:
            v = jnp.where(vmask, v, 0.0)
        v = v.astype(BF16)
        st = st_ref[h]
        o = _dot(a.astype(BF16), v) + _dot_nt(qt[:, ks], st.astype(BF16))
        st_ref[h] = st * dec[:, ks] + _dot_tn(v, ke[:, ks])
        on = o * lax.rsqrt(jnp.mean(o * o, -1, keepdims=True) + NORM_EPS) * norm_ref[...]
        o_ref[:, vs] = (on * _silu(g_ref[:, vs])).astype(o_ref.dtype)

    @pl.when(ci == n - 1)
    def _():
        for h in range(N_HEADS_B):
            sout_ref[h] = st_ref[h].T


def _gla(proj, wgate, bgate, norm, state, l, nb_seq, c, n, valid, out_dtype):
    m = nb_seq * c * n
    row = lambda b, ci: b * n + ci
    hk = N_HEADS_B * DK_B
    in_specs = [
        pl.BlockSpec((c, hk), lambda b, ci: (row(b, ci), COL_BQ // hk)),
        pl.BlockSpec((c, hk), lambda b, ci: (row(b, ci), COL_BK // hk)),
        pl.BlockSpec((c, MIX_W), lambda b, ci: (row(b, ci), COL_BV // MIX_W)),
        pl.BlockSpec((c, MIX_W), lambda b, ci: (row(b, ci), COL_BG // MIX_W)),
        pl.BlockSpec((c, LANES), lambda b, ci: (row(b, ci), COL_SM // LANES)),
        pl.BlockSpec((None, LANES, hk), lambda b, ci: (l, 0, 0)),
        pl.BlockSpec((None, 1, hk), lambda b, ci: (l, 0, 0)),
        pl.BlockSpec((None, 1, DV_B), lambda b, ci: (l, 0, 0)),
    ]
    args = [proj, proj, proj, proj, proj, wgate, bgate, norm]
    if state is not None:
        in_specs.append(pl.BlockSpec((None, None, N_HEADS_B, DK_B, DV_B), lambda b, ci: (l, b, 0, 0, 0)))
        args.append(state)
    return pl.pallas_call(
        functools.partial(_gla_kernel, c=c, n=n, has_state=state is not None, valid=valid),
        grid=(nb_seq, n),
        in_specs=in_specs,
        out_specs=[
            pl.BlockSpec((c, MIX_W), lambda b, ci: (row(b, ci), 0)),
            pl.BlockSpec((None, N_HEADS_B, DK_B, DV_B), lambda b, ci: (b, 0, 0, 0)),
        ],
        out_shape=[
            jax.ShapeDtypeStruct((m, MIX_W), out_dtype),
            jax.ShapeDtypeStruct((nb_seq, N_HEADS_B, DK_B, DV_B), F32),
        ],
        scratch_shapes=[pltpu.VMEM((N_HEADS_B, DV_B, DK_B), F32)],
        compiler_params=_cparams("parallel", "arbitrary"),
        name="gla",
    )(*args)


def _unit_lower_inverse(a, c):
    i = lax.broadcasted_iota(jnp.int32, (c, c), 0)
    j = lax.broadcasted_iota(jnp.int32, (c, c), 1)
    inv = jnp.where(i == j, 1.0, 0.0) - a
    pw = a
    span = 2
    while span < c:
        pw = jnp.dot(pw, pw, precision=HIGHEST, preferred_element_type=F32)
        inv = inv + jnp.dot(inv, pw, precision=HIGHEST, preferred_element_type=F32)
        span *= 2
    return inv


def _dn_kernel(*refs, c, n, has_state, valid):
    if has_state:
        (x_ref, z_ref, sm_ref, cw_ref, alog_ref, dt_ref, norm_ref, s0_ref, cst_ref,
         o_ref, sout_ref, s_ref, xx_ref, y_ref) = refs
    else:
        (x_ref, z_ref, sm_ref, cw_ref, alog_ref, dt_ref, norm_ref,
         o_ref, sout_ref, s_ref, xx_ref, y_ref) = refs
    ci = pl.program_id(1)
    halo = SUBLANES

    @pl.when(ci == 0)
    def _():
        if has_state:
            s_ref[...] = s0_ref[...]
        else:
            s_ref[...] = jnp.zeros_like(s_ref)
        xx_ref[0:halo, :] = jnp.zeros((halo, QKV_C), F32)

    x = x_ref[...]
    xx_ref[halo:halo + c, :] = x
    if has_state:
        xx_ref[halo:halo + CONV_C - 1, :] = cst_ref[...]
    cw = cw_ref[...]
    y = xx_ref[halo - 3:halo - 3 + c, :] * cw[0:1]
    for t in range(1, CONV_C):
        y = y + xx_ref[halo - 3 + t:halo - 3 + t + c, :] * cw[t:t + 1]
    y_ref[...] = _silu(y)
    xx_ref[0:halo, :] = x[c - halo:c]

    vmask = _row_mask(c, valid) if valid else None
    lane = lax.broadcasted_iota(jnp.int32, (1, LANES), 1)
    sm = sm_ref[...]
    g_all = -jnp.exp(alog_ref[...]) * _softplus(sm + dt_ref[...])
    g_all = jnp.where((lane >= LANE_CA) & (lane < LANE_CA + N_HEADS_C), g_all, 0.0)
    beta_all = _sigmoid(sm)
    if valid:
        g_all = jnp.where(vmask, g_all, 0.0)
        beta_all = jnp.where(vmask, beta_all, 0.0)
    i = lax.broadcasted_iota(jnp.int32, (c, c), 0)
    j = lax.broadcasted_iota(jnp.int32, (c, c), 1)
    gc = jnp.dot((i >= j).astype(F32), g_all, precision=HIGHEST, preferred_element_type=F32)
    gct = gc.T
    glast = gc[c - 1:c, :]
    eg = jnp.exp(gc)
    eend = jnp.exp(glast - gc)
    dec = jnp.exp(glast)
    for h in range(N_HEADS_C):
        qs = slice(DK_C * h, DK_C * (h + 1))
        ks = slice(N_HEADS_C * DK_C + DK_C * h, N_HEADS_C * DK_C + DK_C * (h + 1))
        vs = slice(2 * N_HEADS_C * DK_C + DV_C * h, 2 * N_HEADS_C * DK_C + DV_C * (h + 1))
        gl = LANE_CA + h
        bl = LANE_CB + h
        qh = y_ref[:, qs]
        kh = y_ref[:, ks]
        vh = y_ref[:, vs]
        qh = qh * lax.rsqrt(jnp.sum(qh * qh, -1, keepdims=True) + NORM_EPS) * DK_C ** -0.5
        kh = kh * lax.rsqrt(jnp.sum(kh * kh, -1, keepdims=True) + NORM_EPS)
        if valid:
            kh = jnp.where(vmask, kh, 0.0)
            vh = jnp.where(vmask, vh, 0.0)
        beta = beta_all[:, bl:bl + 1]
        gcol = gc[:, gl:gl + 1]
        grow = gct[gl:gl + 1, :]
        e = jnp.exp(jnp.where(i >= j, gcol - grow, -jnp.inf))
        kb = kh.astype(BF16)
        a = beta * _dot_nt(kb, kb) * jnp.where(i > j, e, 0.0)
        rhs = jnp.concatenate([vh * beta, kh * (beta * eg[:, gl:gl + 1])], axis=1)
        sol = jnp.dot(_unit_lower_inverse(a, c), rhs, precision=HIGHEST, preferred_element_type=F32)
        ub = sol[:, :DV_C]
        w = sol[:, DV_C:]
        qb = qh.astype(BF16)
        aqk = _dot_nt(qb, kb) * e
        qd = (qh * eg[:, gl:gl + 1]).astype(BF16)
        ke = (kh * eend[:, gl:gl + 1]).astype(BF16)
        s = s_ref[h]
        sb = s.astype(BF16)
        u = ub - _dot(w.astype(BF16), sb)
        ubf = u.astype(BF16)
        o = _dot(qd, sb) + _dot(aqk.astype(BF16), ubf)
        s_ref[h] = s * dec[:, gl:gl + 1] + _dot_tn(ke, ubf)
        on = o * lax.rsqrt(jnp.mean(o * o, -1, keepdims=True) + NORM_EPS) * norm_ref[...]
        o_ref[:, DV_C * h:DV_C * (h + 1)] = (on * _silu(z_ref[:, DV_C * h:DV_C * (h + 1)])).astype(o_ref.dtype)

    @pl.when(ci == n - 1)
    def _():
        sout_ref[...] = s_ref[...]


def _deltanet(proj, conv_w, alog, dtb, norm, state, conv_state, l, nb_seq, c, n, valid, out_dtype):
    m = nb_seq * c * n
    row = lambda b, ci: b * n + ci
    in_specs = [
        pl.BlockSpec((c, QKV_C), lambda b, ci: (row(b, ci), COL_CQKV // QKV_C)),
        pl.BlockSpec((c, MIX_W), lambda b, ci: (row(b, ci), COL_CZ // MIX_W)),
        pl.BlockSpec((c, LANES), lambda b, ci: (row(b, ci), COL_SM // LANES)),
        pl.BlockSpec((None, CONV_C, QKV_C), lambda b, ci: (l, 0, 0)),
        pl.BlockSpec((None, 1, LANES), lambda b, ci: (l, 0, 0)),
        pl.BlockSpec((None, 1, LANES), lambda b, ci: (l, 0, 0)),
        pl.BlockSpec((None, 1, DV_C), lambda b, ci: (l, 0, 0)),
    ]
    args = [proj, proj, proj, conv_w, alog, dtb, norm]
    has_state = state is not None
    if has_state:
        assert n == 1
        in_specs.append(pl.BlockSpec((None, None, N_HEADS_C, DK_C, DV_C), lambda b, ci: (l, b, 0, 0, 0)))
        in_specs.append(pl.BlockSpec((None, None, CONV_C - 1, QKV_C), lambda b, ci: (l, b, 0, 0)))
        args += [state, conv_state]
    return pl.pallas_call(
        functools.partial(_dn_kernel, c=c, n=n, has_state=has_state, valid=valid),
        grid=(nb_seq, n),
        in_specs=in_specs,
        out_specs=[
            pl.BlockSpec((c, MIX_W), lambda b, ci: (row(b, ci), 0)),
            pl.BlockSpec((None, N_HEADS_C, DK_C, DV_C), lambda b, ci: (b, 0, 0, 0)),
        ],
        out_shape=[
            jax.ShapeDtypeStruct((m, MIX_W), out_dtype),
            jax.ShapeDtypeStruct((nb_seq, N_HEADS_C, DK_C, DV_C), F32),
        ],
        scratch_shapes=[
            pltpu.VMEM((N_HEADS_C, DK_C, DV_C), F32),
            pltpu.VMEM((SUBLANES + c, QKV_C), F32),
            pltpu.VMEM((c, QKV_C), F32),
        ],
        compiler_params=_cparams("parallel", "arbitrary"),
        name="deltanet",
    )(*args)


def _merge_kernel(a_ref, b_ref, c_ref, wa_ref, wb_ref, wc_ref, ga_ref, gb_ref, gc_ref, o_ref):
    acc = _sigmoid(ga_ref[...]) * _dot(a_ref[...].astype(BF16), wa_ref[...])
    acc = acc + _sigmoid(gb_ref[...]) * _dot(b_ref[...].astype(BF16), wb_ref[...])
    acc = acc + _sigmoid(gc_ref[...]) * _dot(c_ref[...].astype(BF16), wc_ref[...])
    o_ref[...] = acc.astype(o_ref.dtype)


def _merge(oa, ob, oc, proj, wbr, l, tm, tn):
    m = oa.shape[0]
    nj = D_MODEL // tn
    br = pl.BlockSpec((tm, MIX_W), lambda i, j: (i, 0))
    wspec = lambda nbr: pl.BlockSpec((None, None, MIX_W, tn), lambda i, j: (l, nbr, 0, j))
    gspec = lambda nbr: pl.BlockSpec((tm, tn), lambda i, j: (i, COL_MG // tn + nbr * nj + j))
    return pl.pallas_call(
        _merge_kernel,
        grid=(m // tm, nj),
        in_specs=[br, br, br, wspec(0), wspec(1), wspec(2), gspec(0), gspec(1), gspec(2)],
        out_specs=pl.BlockSpec((tm, tn), lambda i, j: (i, j)),
        out_shape=jax.ShapeDtypeStruct((m, D_MODEL), BF16),
        compiler_params=_cparams("parallel", "parallel"),
        name="merge",
    )(oa, ob, oc, wbr, wbr, wbr, proj, proj, proj)


def _outproj_kernel(a_ref, w_ref, x_ref, g_ref, b_ref, h_ref, hb_ref):
    y = ALPHA * x_ref[...] + _dot(a_ref[...], w_ref[...])
    h = _layer_norm(y, g_ref[...], b_ref[...])
    h_ref[...] = h
    hb_ref[...] = h.astype(BF16)


def _outproj(mixed, wo, x, g, b, l, tm):
    m = x.shape[0]
    vec = pl.BlockSpec((None, 1, D_MODEL), lambda i: (l, 0, 0))
    return pl.pallas_call(
        _outproj_kernel,
        grid=(m // tm,),
        in_specs=[pl.BlockSpec((tm, D_MODEL), lambda i: (i, 0)),
                  pl.BlockSpec((None, D_MODEL, D_MODEL), lambda i: (l, 0, 0)),
                  pl.BlockSpec((tm, D_MODEL), lambda i: (i, 0)), vec, vec],
        out_specs=[pl.BlockSpec((tm, D_MODEL), lambda i: (i, 0)),
                   pl.BlockSpec((tm, D_MODEL), lambda i: (i, 0))],
        out_shape=[jax.ShapeDtypeStruct((m, D_MODEL), F32), jax.ShapeDtypeStruct((m, D_MODEL), BF16)],
        compiler_params=_cparams("parallel"),
        name="outproj_ln",
    )(mixed, wo, x, g, b)


FFN_HALO = 16


def _ffn_up_kernel(*refs, tm, tail, tiles_per_seq, has_state):
    if has_state:
        (a_ref, ah_ref, wg_ref, wp_ref, cg_ref, cp_ref, sg_ref, sp_ref,
         act_ref, tg_ref, tp_ref, eg_ref, ep_ref) = refs
    else:
        (a_ref, ah_ref, wg_ref, wp_ref, cg_ref, cp_ref,
         act_ref, tg_ref, tp_ref, eg_ref, ep_ref) = refs
    first = (pl.program_id(0) % tiles_per_seq) == 0
    a = a_ref[...]
    ah = ah_ref[...]
    rg = lax.broadcasted_iota(jnp.int32, (tm, 1), 0) % GROUP
    inject = (rg >= TOK0 - (FFN_CONV - 1)) & (rg < TOK0)

    def branch(w_ref, c_ref, st_ref, ext_ref, tail_ref):
        w = w_ref[...]
        u = _dot(a, w)
        if has_state:
            u = jnp.where(inject, st_ref[...], u)
        uh = _dot(ah, w)
        ext_ref[0:FFN_HALO, :] = jnp.where(first, 0.0, uh)
        ext_ref[FFN_HALO:FFN_HALO + tm, :] = u
        tail_ref[...] = u[tm - tail:tm]
        cw = c_ref[...]
        y = ext_ref[FFN_HALO - 2:FFN_HALO - 2 + tm, :] * cw[0:1]
        y = y + ext_ref[FFN_HALO - 1:FFN_HALO - 1 + tm, :] * cw[1:2]
        return y + u * cw[2:3]

    gate = branch(wg_ref, cg_ref, sg_ref if has_state else None, eg_ref, tg_ref)
    up = branch(wp_ref, cp_ref, sp_ref if has_state else None, ep_ref, tp_ref)
    act_ref[...] = (_silu(gate) * up).astype(act_ref.dtype)


def _ffn_up(hb, wu, cw, state, l, tm, tn, tail, tiles_per_seq):
    m = hb.shape[0]
    nj = D_FF // tn
    hal = tm // FFN_HALO
    in_specs = [
        pl.BlockSpec((tm, D_MODEL), lambda i, j: (i, 0)),
        pl.BlockSpec((FFN_HALO, D_MODEL), lambda i, j: (jnp.maximum(i * hal - 1, 0), 0)),
        pl.BlockSpec((None, D_MODEL, tn), lambda i, j: (l, 0, j)),
        pl.BlockSpec((None, D_MODEL, tn), lambda i, j: (l, 0, nj + j)),
        pl.BlockSpec((None, FFN_CONV, tn), lambda i, j: (l, 0, j)),
        pl.BlockSpec((None, FFN_CONV, tn), lambda i, j: (l, 0, nj + j)),
    ]
    args = [hb, hb, wu, wu, cw, cw]
    if state is not None:
        in_specs += [pl.BlockSpec((None, tm, tn), lambda i, j: (l, i, j)),
                     pl.BlockSpec((None, tm, tn), lambda i, j: (l, i, nj + j))]
        args += [state, state]
    return pl.pallas_call(
        functools.partial(_ffn_up_kernel, tm=tm, tail=tail, tiles_per_seq=tiles_per_seq,
                          has_state=state is not None),
        grid=(m // tm, nj),
        in_specs=in_specs,
        out_specs=[pl.BlockSpec((tm, tn), lambda i, j: (i, j)),
                   pl.BlockSpec((tail, tn), lambda i, j: (i, j)),
                   pl.BlockSpec((tail, tn), lambda i, j: (i, j))],
        out_shape=[jax.ShapeDtypeStruct((m, D_FF), BF16),
                   jax.ShapeDtypeStruct((m // tm * tail, D_FF), F32),
                   jax.ShapeDtypeStruct((m // tm * tail, D_FF), F32)],
        scratch_shapes=[pltpu.VMEM((FFN_HALO + tm, tn), F32), pltpu.VMEM((FFN_HALO + tm, tn), F32)],
        compiler_params=_cparams("parallel", "parallel"),
        name="ffn_up",
    )(*args)


def _ffn_down_kernel(a_ref, w_ref, h_ref, g_ref, b_ref, y_ref, yb_ref, acc_ref):
    kk = pl.program_id(1)

    @pl.when(kk == 0)
    def _():
        acc_ref[...] = jnp.zeros_like(acc_ref)

    acc_ref[...] += _dot(a_ref[...], w_ref[...])

    @pl.when(kk == pl.num_programs(1) - 1)
    def _():
        y = _layer_norm(ALPHA * h_ref[...] + acc_ref[...], g_ref[...], b_ref[...])
        y_ref[...] = y
        yb_ref[...] = y.astype(BF16)


def _ffn_down(act, wd, h, g, b, l, tm, tk):
    m = h.shape[0]
    vec = pl.BlockSpec((None, 1, D_MODEL), lambda i, k: (l, 0, 0))
    return pl.pallas_call(
        _ffn_down_kernel,
        grid=(m // tm, D_FF // tk),
        in_specs=[pl.BlockSpec((tm, tk), lambda i, k: (i, k)),
                  pl.BlockSpec((None, tk, D_MODEL), lambda i, k: (l, k, 0)),
                  pl.BlockSpec((tm, D_MODEL), lambda i, k: (i, 0)), vec, vec],
        out_specs=[pl.BlockSpec((tm, D_MODEL), lambda i, k: (i, 0)),
                   pl.BlockSpec((tm, D_MODEL), lambda i, k: (i, 0))],
        out_shape=[jax.ShapeDtypeStruct((m, D_MODEL), F32), jax.ShapeDtypeStruct((m, D_MODEL), BF16)],
        scratch_shapes=[pltpu.VMEM((tm, D_MODEL), F32)],
        compiler_params=_cparams("parallel", "arbitrary"),
        name="ffn_down_ln",
    )(act, wd, h, g, b)


def _pack_w_in(w_in):
    idx = np.cumsum(IN_SPLITS)[:-1].tolist()
    aq, ak, av, bq, bk, bv, bg, blr, cqkv, cz, ca, cb, mg = jnp.split(w_in, idx, axis=-1)
    used = COL_SM + GATE_RANK + 2 * N_HEADS_C
    pad = jnp.zeros(w_in.shape[:-1] + (N_PROJ - used,), w_in.dtype)
    return jnp.concatenate([mg, cqkv, aq, bv, bg, cz, bq, bk, ak, av, blr, ca, cb, pad], -1).astype(BF16)


def _lane_vec(v, lane0):
    n = v.shape[-1]
    return jnp.pad(v, ((0, 0), (lane0, LANES - lane0 - n)))[:, None, :]


class _Cfg:
    def __init__(self, nb_seq, t, c, valid, mm_tm, merge_tm, out_tm, up_tm, down_tm, act_dtype):
        self.nb_seq, self.t, self.c, self.valid = nb_seq, t, c, valid
        self.n = t // c
        self.mm_tm, self.merge_tm, self.out_tm, self.up_tm, self.down_tm = mm_tm, merge_tm, out_tm, up_tm, down_tm
        self.act_dtype = act_dtype


PROMPT_CFG = _Cfg(BATCH, SEQ, CHUNK, None, 1024, 1024, 256, 1024, 512, BF16)
DECODE_CFG = _Cfg(DEC_BATCH, GROUP, GROUP, (TOK0, TOK0 + DEC_SEQ), DEC_ROWS, DEC_ROWS, DEC_ROWS, DEC_ROWS, DEC_ROWS, F32)
MM_TN = 1024
MERGE_TN = 512
UP_TN = 512
DOWN_TK = 512


def _layer(cfg, l, x, xb, w, tabs, st):
    proj = _matmul(xb, w["w_in"], l, cfg.mm_tm, MM_TN)
    if st is None:
        oa, new_k, new_v = _swa_prompt(proj, tabs, w["sinks"], l, cfg.nb_seq, cfg.t)
    else:
        oa, new_k, new_v = _swa_decode(proj, tabs, w["sinks"], st["swa_k"], st["swa_v"], l)
    ob, s_b = _gla(proj, w["wgate"], w["bgate"], w["gla_norm"], None if st is None else st["gla"],
                   l, cfg.nb_seq, cfg.c, cfg.n, cfg.valid, cfg.act_dtype)
    oc, s_c = _deltanet(proj, w["dn_conv"], w["alog"], w["dtb"], w["dn_norm"],
                        None if st is None else st["delta"], None if st is None else st["dconv"],
                        l, cfg.nb_seq, cfg.c, cfg.n, cfg.valid, cfg.act_dtype)
    mixed = _merge(oa, ob, oc, proj, w["w_branch"], l, cfg.merge_tm, MERGE_TN)
    h, hb = _outproj(mixed, w["w_out"], x, w["ln1_g"], w["ln1_b"], l, cfg.out_tm)
    tail = SUBLANES if st is None else cfg.up_tm
    act, tg, tp = _ffn_up(hb, w["w_up"], w["ffn_conv"], None if st is None else st["fconv"],
                          l, cfg.up_tm, UP_TN, tail, cfg.t // cfg.up_tm if st is None else 1)
    y, yb = _ffn_down(act, w["w_down"], h, w["ln2_g"], w["ln2_b"], l, cfg.down_tm, DOWN_TK)
    return y, yb, (new_k, new_v, s_b, s_c, proj, tg, tp)


def kernel(x_prompt, x_sample, cache_swa_k, cache_swa_v, state_gla, state_delta, state_delta_conv,
           state_ffn_conv, w_in, attn_sinks, w_gla_gate, b_gla_gate, gla_norm, dn_conv, dn_a_log,
           dn_dt_bias, dn_norm, w_branch, w_out, ln1_g, ln1_b, w_up, ffn_conv, w_down, ln2_g, ln2_b):
    cw = cache_swa_k.shape[2]
    w = {
        "w_in": _pack_w_in(w_in),
        "sinks": attn_sinks,
        "wgate": jnp.pad(w_gla_gate, ((0, 0), (LANE_BLR, LANES - LANE_BLR - GATE_RANK), (0, 0))).astype(BF16),
        "bgate": b_gla_gate[:, None, :],
        "gla_norm": gla_norm[:, None, :],
        "dn_conv": dn_conv,
        "alog": _lane_vec(dn_a_log, LANE_CA),
        "dtb": _lane_vec(dn_dt_bias, LANE_CA),
        "dn_norm": dn_norm[:, None, :],
        "w_branch": w_branch.astype(BF16),
        "w_out": w_out.astype(BF16),
        "ln1_g": ln1_g[:, None, :], "ln1_b": ln1_b[:, None, :],
        "w_up": w_up.astype(BF16),
        "ffn_conv": ffn_conv,
        "w_down": w_down.astype(BF16),
        "ln2_g": ln2_g[:, None, :], "ln2_b": ln2_b[:, None, :],
    }
    tabs_p = _rope_tables(jnp.arange(SEQ))
    tabs_s = _rope_tables(PAST_LEN + jnp.arange(GROUP) - TOK0)
    st = {
        "swa_k": cache_swa_k.reshape(DEPTH, DEC_BATCH, cw, KV_W),
        "swa_v": cache_swa_v.reshape(DEPTH, DEC_BATCH, cw, KV_W),
        "gla": state_gla,
        "delta": state_delta,
        "dconv": state_delta_conv,
        "fconv": jnp.pad(state_ffn_conv, ((0, 0), (0, 0), (TOK0 - (FFN_CONV - 1), GROUP - TOK0), (0, 0))
                         ).reshape(DEPTH, DEC_ROWS, 2 * D_FF),
    }
    xp = x_prompt.reshape(BATCH * SEQ, D_MODEL)
    xs = jnp.pad(x_sample, ((0, 0), (TOK0, GROUP - TOK0 - DEC_SEQ), (0, 0))).reshape(DEC_ROWS, D_MODEL)
    xpb, xsb = xp.astype(BF16), xs.astype(BF16)
    new_p, new_s = [], []
    tiles = SEQ // PROMPT_CFG.up_tm
    for l in range(DEPTH):
        xp, xpb, (pk, pv, pg, pd, proj, tg, tp) = _layer(PROMPT_CFG, l, xp, xpb, w, tabs_p, None)
        pr = proj.reshape(BATCH, SEQ, N_PROJ)
        tail = jnp.concatenate([tg, tp], -1).reshape(BATCH, tiles, SUBLANES, 2 * D_FF)
        new_p.append((pk.reshape(BATCH, WINDOW, N_KV_A, HEAD_DIM_A), pv.reshape(BATCH, WINDOW, N_KV_A, HEAD_DIM_A),
                      pg, pd, pr[:, SEQ - (CONV_C - 1):, COL_CQKV:COL_CQKV + QKV_C],
                      tail[:, tiles - 1, SUBLANES - (FFN_CONV - 1):]))
        xs, xsb, (sk, sv, sg, sd, proj, tg, tp) = _layer(DECODE_CFG, l, xs, xsb, w, tabs_s, st)
        pr = proj.reshape(DEC_BATCH, GROUP, N_PROJ)
        tail = jnp.concatenate([tg, tp], -1).reshape(DEC_BATCH, GROUP, 2 * D_FF)
        hi = TOK0 + DEC_SEQ
        new_s.append((sk.reshape(DEC_BATCH, cw, N_KV_A, HEAD_DIM_A), sv.reshape(DEC_BATCH, cw, N_KV_A, HEAD_DIM_A),
                      sg, sd, pr[:, hi - (CONV_C - 1):hi, COL_CQKV:COL_CQKV + QKV_C],
                      tail[:, hi - (FFN_CONV - 1):hi]))
    p_k, p_v, p_gla, p_delta, p_dconv, p_fconv = (jnp.stack(a) for a in zip(*new_p))
    s_k, s_v, s_gla, s_delta, s_dconv, s_fconv = (jnp.stack(a) for a in zip(*new_s))
    y_p = xp.reshape(BATCH, SEQ, D_MODEL)
    y_s = xs.reshape(DEC_BATCH, GROUP, D_MODEL)[:, TOK0:TOK0 + DEC_SEQ]
    return (y_p, y_s, p_k, p_v, p_gla, p_delta, p_dconv, p_fconv,
            s_k, s_v, s_gla, s_delta, s_dconv, s_fconv)
```
